```python
import jax, jax.numpy as jnp
from jax import lax
import numpy as np

D_MODEL = 2048
BATCH = 32
SEQ = 256
DEPTH = 4
DEC_BATCH = 4
DEC_SEQ = 2048
PAST_LEN = 256

GRID_W = 64
WIN_H = 8
WIN_W = 16
Q_BLK_W = 16
K_BLK_W = 32
N_COL_BLK = GRID_W // Q_BLK_W
ATT_WIDTH = D_MODEL // 2
N_HEADS = 8
HEAD_DIM = ATT_WIDTH // N_HEADS
CONV_WIDTH = D_MODEL // 4
CONV_K = 31
POOL_WIDTH = D_MODEL - ATT_WIDTH - CONV_WIDTH
POOL_WINDOWS = (2, 4, 8, 16)
N_POOL_GROUPS = len(POOL_WINDOWS)
POOL_GROUP = POOL_WIDTH // N_POOL_GROUPS
D_IN = 3 * ATT_WIDTH + 2 * CONV_WIDTH + POOL_WIDTH
D_FF = -(-8 * D_MODEL // (3 * 256)) * 256
N_MOD = 6
Q_BLOCK = 128
EPS = 1e-6
NEG = -1e30

kernel_name = 'hybrid_natten_conformer_pool_diffusion_step'


def rmsnorm(x, g):
    xf = x.astype(jnp.float32)
    y = xf * lax.rsqrt(jnp.mean(xf * xf, axis=-1, keepdims=True) + EPS)
    return (y * g.astype(jnp.float32)).astype(x.dtype)


def layernorm(x, g, b):
    xf = x.astype(jnp.float32)
    mu = jnp.mean(xf, axis=-1, keepdims=True)
    var = jnp.mean(jnp.square(xf - mu), axis=-1, keepdims=True)
    y = (xf - mu) * lax.rsqrt(var + EPS)
    return (y * g.astype(jnp.float32) + b.astype(jnp.float32)).astype(x.dtype)


def modulation(cv, w_ada, b_ada):
    m = (jax.nn.silu(cv) @ w_ada + b_ada)[:, None, :]
    return jnp.split(m, N_MOD, axis=-1)


def dense_attention(q, k, v):
    B, T, H, Dh = q.shape
    qb = (q * Dh ** -0.5).reshape(B, T // Q_BLOCK, Q_BLOCK, H, Dh).transpose(1, 0, 2, 3, 4)

    def blk(qi):
        s = jnp.einsum('bqhd,bkhd->bhqk', qi, k).astype(jnp.float32)
        p = jax.nn.softmax(s, axis=-1).astype(v.dtype)
        return jnp.einsum('bhqk,bkhd->bqhd', p, v)

    o = lax.map(blk, qb)
    return o.transpose(1, 0, 2, 3, 4).reshape(B, T, H * Dh)


def neighbourhood_attention(q, k, v, k_ctx, v_ctx, rpb):
    B, N, H, Dh = q.shape
    rows = N // GRID_W
    kh = min(WIN_H, rows)
    qg = (q * Dh ** -0.5).reshape(B, rows, N_COL_BLK, Q_BLK_W, H, Dh).transpose(1, 0, 2, 3, 4, 5)
    kg = k.reshape(B, rows, GRID_W, H, Dh)
    vg = v.reshape(B, rows, GRID_W, H, Dh)
    qcol = jnp.arange(GRID_W).reshape(N_COL_BLK, Q_BLK_W)
    kc0 = jnp.clip(jnp.arange(N_COL_BLK) * Q_BLK_W - WIN_W // 2, 0, GRID_W - K_BLK_W)
    kcol = kc0[:, None] + jnp.arange(K_BLK_W)
    qstart = jnp.clip(qcol - WIN_W // 2, 0, GRID_W - WIN_W)
    kc = kcol[:, None, :]
    col_valid = (kc >= qstart[..., None]) & (kc < qstart[..., None] + WIN_W)
    col_idx = jnp.clip(kc - qcol[..., None], -(WIN_W - 1), WIN_W - 1) + WIN_W - 1
    nwin = kh * K_BLK_W

    def row_fn(args):
        r, q_r = args
        sr = jnp.clip(r - kh // 2, 0, rows - kh)
        k_win = lax.dynamic_slice_in_dim(kg, sr, kh, axis=1)[:, :, kcol]
        v_win = lax.dynamic_slice_in_dim(vg, sr, kh, axis=1)[:, :, kcol]
        s_win = jnp.einsum('bnqhd,bjnmhd->bhnqjm', q_r, k_win).astype(jnp.float32)
        row_idx = sr + jnp.arange(kh) - r + WIN_H - 1
        bias = rpb[:, row_idx][:, :, col_idx].transpose(0, 2, 3, 1, 4).astype(jnp.float32)
        s_win = jnp.where(col_valid[None, None, :, :, None, :], s_win + bias[None], NEG)
        s_ctx = jnp.einsum('bnqhd,blhd->bhnql', q_r, k_ctx).astype(jnp.float32)
        s = jnp.concatenate([s_win.reshape(B, H, N_COL_BLK, Q_BLK_W, nwin), s_ctx], axis=-1)
        p = jax.nn.softmax(s, axis=-1).astype(v.dtype)
        p_win = p[..., :nwin].reshape(B, H, N_COL_BLK, Q_BLK_W, kh, K_BLK_W)
        p_ctx = p[..., nwin:]
        o = (jnp.einsum('bhnqjm,bjnmhd->bnqhd', p_win, v_win)
             + jnp.einsum('bhnql,blhd->bnqhd', p_ctx, v_ctx))
        return o.reshape(B, GRID_W, H * Dh)

    o = lax.map(row_fn, (jnp.arange(rows), qg))
    return o.transpose(1, 0, 2, 3).reshape(B, N, H * Dh)


def conv_module(u, w_dw, b_dw, ln_g, ln_b, w_pw, b_pw):
    a, g = jnp.split(u, 2, axis=-1)
    h = a * jax.nn.sigmoid(g)
    h = lax.conv_general_dilated(h, w_dw, window_strides=(1,), padding=[(CONV_K // 2, CONV_K // 2)],
                                 dimension_numbers=('NWC', 'WIO', 'NWC'),
                                 feature_group_count=CONV_WIDTH) + b_dw
    h = jax.nn.silu(layernorm(h, ln_g, ln_b))
    return h @ w_pw + b_pw


def pool_mixer(u, w_pool, pool_scale):
    B, T, C = u.shape
    uf = u.astype(jnp.float32)
    cs = jnp.concatenate([jnp.zeros((B, 1, C), jnp.float32), jnp.cumsum(uf, axis=1)], axis=1)
    t = jnp.arange(T)
    outs = []
    for gi, w in enumerate(POOL_WINDOWS):
        lo = jnp.clip(t - w // 2, 0, T)
        hi = jnp.clip(t - w // 2 + w, 0, T)
        csg = cs[..., gi * POOL_GROUP:(gi + 1) * POOL_GROUP]
        mean = (csg[:, hi] - csg[:, lo]) / (hi - lo).astype(jnp.float32)[None, :, None]
        outs.append(mean - uf[..., gi * POOL_GROUP:(gi + 1) * POOL_GROUP])
    d = jnp.stack(outs, axis=2).astype(u.dtype)
    y = jnp.einsum('btgi,gio->btgo', d, w_pool).reshape(B, T, C)
    return y * pool_scale


def trunk_layer(x, mod, attend, g_pre_mix, g_post_mix, g_pre_ffn, g_post_ffn, w_in,
                w_dw, b_dw, ln_g, ln_b, w_pw, b_pw, w_pool, pool_scale, w_out, w_ffn_in, w_ffn_out):
    sh1, sc1, g1, sh2, sc2, g2 = mod
    B, T, _ = x.shape
    h = rmsnorm(x, g_pre_mix) * (1 + sc1) + sh1
    z = h @ w_in
    q, k, v, u_conv, u_pool = jnp.split(
        z, [ATT_WIDTH, 2 * ATT_WIDTH, 3 * ATT_WIDTH, 3 * ATT_WIDTH + 2 * CONV_WIDTH], axis=-1)
    q = q.reshape(B, T, N_HEADS, HEAD_DIM)
    k = k.reshape(B, T, N_HEADS, HEAD_DIM)
    v = v.reshape(B, T, N_HEADS, HEAD_DIM)
    a = attend(q, k, v)
    cm = conv_module(u_conv, w_dw, b_dw, ln_g, ln_b, w_pw, b_pw)
    pm = pool_mixer(u_pool, w_pool, pool_scale)
    o = jnp.concatenate([a, cm, pm], axis=-1) @ w_out
    x = x + g1 * rmsnorm(o, g_post_mix)
    h = rmsnorm(x, g_pre_ffn) * (1 + sc2) + sh2
    gt, up = jnp.split(h @ w_ffn_in, 2, axis=-1)
    f = (jax.nn.silu(gt) * up) @ w_ffn_out
    x = x + g2 * rmsnorm(f, g_post_ffn)
    return x, k, v


def setup_inputs(seed: int = 0) -> dict:
    key = jax.random.key(seed)
    ks = jax.random.split(key, 32)
    f32 = jnp.float32
    nrm = lambda k, shape, s: jax.random.normal(k, shape, f32) * s
    gain = lambda k, shape: 1.0 + 0.02 * jax.random.normal(k, shape, f32)
    return {
        'x_prompt': nrm(ks[0], (BATCH, SEQ, D_MODEL), 1.0),
        'x_sample': nrm(ks[1], (DEC_BATCH, DEC_SEQ, D_MODEL), 1.0),
        'cache_k': nrm(ks[2], (DEC_BATCH, DEPTH, PAST_LEN, N_HEADS, HEAD_DIM), 1.0),
        'cache_v': nrm(ks[3], (DEC_BATCH, DEPTH, PAST_LEN, N_HEADS, HEAD_DIM), 1.0),
        'c': nrm(ks[4], (DEC_BATCH, D_MODEL), 1.0),
        'c_ctx': nrm(ks[5], (D_MODEL,), 1.0),
        'w_ada': nrm(ks[6], (DEPTH, D_MODEL, N_MOD * D_MODEL), 0.5 * D_MODEL ** -0.5),
        'b_ada': nrm(ks[7], (DEPTH, N_MOD * D_MODEL), 0.01),
        'g_pre_mix': gain(ks[8], (DEPTH, D_MODEL)),
        'g_post_mix': gain(ks[9], (DEPTH, D_MODEL)),
        'g_pre_ffn': gain(ks[10], (DEPTH, D_MODEL)),
        'g_post_ffn': gain(ks[11], (DEPTH, D_MODEL)),
        'w_in': nrm(ks[12], (DEPTH, D_MODEL, D_IN), D_MODEL ** -0.5),
        'rpb': nrm(ks[13], (DEPTH, N_HEADS, 2 * WIN_H - 1, 2 * WIN_W - 1), 0.1),
        'w_dw': nrm(ks[14], (DEPTH, CONV_K, 1, CONV_WIDTH), CONV_K ** -0.5),
        'b_dw': nrm(ks[15], (DEPTH, CONV_WIDTH), 0.01),
        'ln_conv_g': gain(ks[16], (DEPTH, CONV_WIDTH)),
        'ln_conv_b': nrm(ks[17], (DEPTH, CONV_WIDTH), 0.01),
        'w_pw': nrm(ks[18], (DEPTH, CONV_WIDTH, CONV_WIDTH), CONV_WIDTH ** -0.5),
        'b_pw': nrm(ks[19], (DEPTH, CONV_WIDTH), 0.01),
        'w_pool': nrm(ks[20], (DEPTH, N_POOL_GROUPS, POOL_GROUP, POOL_GROUP), POOL_GROUP ** -0.5),
        'pool_scale': gain(ks[21], (DEPTH, POOL_WIDTH)),
        'w_out': nrm(ks[22], (DEPTH, D_MODEL, D_MODEL), D_MODEL ** -0.5),
        'w_ffn_in': nrm(ks[23], (DEPTH, D_MODEL, 2 * D_FF), D_MODEL ** -0.5),
        'w_ffn_out': nrm(ks[24], (DEPTH, D_FF, D_MODEL), D_FF ** -0.5),
    }


def reference(x_prompt, x_sample, cache_k, cache_v, c, c_ctx, w_ada, b_ada, g_pre_mix, g_post_mix,
              g_pre_ffn, g_post_ffn, w_in, rpb, w_dw, b_dw, ln_conv_g, ln_conv_b, w_pw, b_pw,
              w_pool, pool_scale, w_out, w_ffn_in, w_ffn_out):
    def params(l):
        return (g_pre_mix[l], g_post_mix[l], g_pre_ffn[l], g_post_ffn[l], w_in[l],
                w_dw[l], b_dw[l], ln_conv_g[l], ln_conv_b[l], w_pw[l], b_pw[l],
                w_pool[l], pool_scale[l], w_out[l], w_ffn_in[l], w_ffn_out[l])

    xp = x_prompt
    ks_new = []
    vs_new = []
    for l in range(DEPTH):
        mod = modulation(c_ctx[None, :], w_ada[l], b_ada[l])
        xp, k_l, v_l = trunk_layer(xp, mod, dense_attention, *params(l))
        ks_new.append(k_l)
        vs_new.append(v_l)
    new_cache_k = jnp.stack(ks_new, axis=1)
    new_cache_v = jnp.stack(vs_new, axis=1)

    xs = x_sample
    for l in range(DEPTH):
        mod = modulation(c, w_ada[l], b_ada[l])
        kc_l = cache_k[:, l]
        vc_l = cache_v[:, l]
        rpb_l = rpb[l]
        attend = lambda q, k, v, kc_l=kc_l, vc_l=vc_l, rpb_l=rpb_l: neighbourhood_attention(
            q, k, v, kc_l, vc_l, rpb_l)
        xs, _, _ = trunk_layer(xs, mod, attend, *params(l))

    return (xp, xs, new_cache_k, new_cache_v)
```

```python
import functools

import jax
import jax.numpy as jnp
from jax import lax
from jax.experimental import pallas as pl
from jax.experimental.pallas import tpu as pltpu

F32 = jnp.float32
BF16 = jnp.bfloat16

D_MODEL = 2048
BATCH = 32
SEQ = 256
DEPTH = 4
DEC_BATCH = 4
DEC_SEQ = 2048
PAST_LEN = 256
GRID_W = 64
GRID_H = DEC_SEQ // GRID_W
WIN_H = 8
WIN_W = 16
ATT_WIDTH = D_MODEL // 2
N_HEADS = 8
HEAD_DIM = ATT_WIDTH // N_HEADS
CONV_WIDTH = D_MODEL // 4
CONV_K = 31
POOL_WIDTH = D_MODEL - ATT_WIDTH - CONV_WIDTH
POOL_WINDOWS = (2, 4, 8, 16)
POOL_GROUP = POOL_WIDTH // len(POOL_WINDOWS)
MIX_IN = 2 * CONV_WIDTH + POOL_WIDTH
D_IN = 3 * ATT_WIDTH + MIX_IN
D_FF = 5632
N_MOD = 6
EPS = 1e-6
NEG = -1e30
Q_SCALE = HEAD_DIM ** -0.5

T_CTX = BATCH * SEQ
T_LAT = DEC_BATCH * DEC_SEQ
T_ALL = T_CTX + T_LAT
MOD_ROWS = 8

VMEM_LIMIT = 56 * 1024 * 1024

TM = 512
TF = 512
MOD_TN = 1024
CHUNK = 256
HALO = 16

Q_ROWS = 4
K_ROWS = 12
N_QBLK = GRID_H // Q_ROWS
QB = Q_ROWS * GRID_W
KB = K_ROWS * GRID_W


def _params(n_axes):
    return pltpu.CompilerParams(
        dimension_semantics=("arbitrary",) * n_axes, vmem_limit_bytes=VMEM_LIMIT)


def _rms(x, g):
    ms = jnp.mean(x * x, axis=-1, keepdims=True)
    return x * lax.rsqrt(ms + EPS) * g


def _silu(x):
    return x * jax.nn.sigmoid(x)


def _mod_row(i, tm):
    return jnp.where(i * tm < T_CTX, 0, 1 + (i * tm - T_CTX) // DEC_SEQ)


def _mod_spec(l, chunk, tm):
    return pl.BlockSpec((None, None, None, 1, D_MODEL),
                        lambda i, *_: (l, _mod_row(i, tm), chunk, 0, 0))


def _layer_vec_spec(l, width):
    return pl.BlockSpec((None, 1, width), lambda *_: (l, 0, 0))


def _mod_kernel(cv_ref, w_ref, b_ref, o_ref):
    s = _silu(cv_ref[...]).astype(BF16)
    o_ref[...] = jnp.dot(s, w_ref[...].astype(BF16), preferred_element_type=F32) + b_ref[...]


def _modulations(cv, w_ada, b_ada):
    n = N_MOD * D_MODEL
    out = pl.pallas_call(
        _mod_kernel,
        grid=(DEPTH, n // MOD_TN),
        in_specs=[
            pl.BlockSpec((MOD_ROWS, D_MODEL), lambda l, j: (0, 0)),
            pl.BlockSpec((None, D_MODEL, MOD_TN), lambda l, j: (l, 0, j)),
            pl.BlockSpec((None, 1, MOD_TN), lambda l, j: (l, 0, j)),
        ],
        out_specs=pl.BlockSpec((None, MOD_ROWS, MOD_TN), lambda l, j: (l, 0, j)),
        out_shape=jax.ShapeDtypeStruct((DEPTH, MOD_ROWS, n), F32),
        compiler_params=_params(2),
    )(cv, w_ada, b_ada.reshape(DEPTH, 1, n))
    return out.reshape(DEPTH, MOD_ROWS, N_MOD, 1, D_MODEL)


def _prenorm_kernel(x_ref, g_ref, sh_ref, sc_ref, h_ref):
    h_ref[...] = (_rms(x_ref[...], g_ref[...]) * (1 + sc_ref[...]) + sh_ref[...]).astype(BF16)


def _prenorm(x, g_pre, mods, l):
    return pl.pallas_call(
        _prenorm_kernel,
        grid=(T_ALL // TM,),
        in_specs=[
            pl.BlockSpec((TM, D_MODEL), lambda i: (i, 0)),
            _layer_vec_spec(l, D_MODEL),
            _mod_spec(l, 0, TM),
            _mod_spec(l, 1, TM),
        ],
        out_specs=pl.BlockSpec((TM, D_MODEL), lambda i: (i, 0)),
        out_shape=jax.ShapeDtypeStruct((T_ALL, D_MODEL), BF16),
        compiler_params=_params(1),
    )(x, g_pre, mods, mods)


def _inproj_kernel(h_ref, w_ref, q_ref, k_ref, v_ref, u_ref, kf_ref, vf_ref):
    i = pl.program_id(0)
    h = h_ref[...]
    q = jnp.dot(h, w_ref[:, 0:ATT_WIDTH], preferred_element_type=F32)
    q_ref[...] = (q * Q_SCALE).astype(BF16)
    k = jnp.dot(h, w_ref[:, ATT_WIDTH:2 * ATT_WIDTH], preferred_element_type=F32)
    k_ref[...] = k.astype(BF16)
    v = jnp.dot(h, w_ref[:, 2 * ATT_WIDTH:3 * ATT_WIDTH], preferred_element_type=F32)
    v_ref[...] = v.astype(BF16)
    u_ref[...] = jnp.dot(h, w_ref[:, 3 * ATT_WIDTH:D_IN], preferred_element_type=F32)

    @pl.when(i < T_CTX // TM)
    def _():
        kf_ref[...] = k
        vf_ref[...] = v


def _inproj(h, w_in, l):
    n_ctx = T_CTX // TM
    row = lambda i: (i, 0)
    ctx_row = lambda i: (jnp.minimum(i, n_ctx - 1), 0)
    return pl.pallas_call(
        _inproj_kernel,
        grid=(T_ALL // TM,),
        in_specs=[
            pl.BlockSpec((TM, D_MODEL), row),
            pl.BlockSpec((None, D_MODEL, D_IN), lambda i: (l, 0, 0),
                         pipeline_mode=pl.Buffered(1)),
        ],
        out_specs=[
            pl.BlockSpec((TM, ATT_WIDTH), row),
            pl.BlockSpec((TM, ATT_WIDTH), row),
            pl.BlockSpec((TM, ATT_WIDTH), row),
            pl.BlockSpec((TM, MIX_IN), row),
            pl.BlockSpec((TM, ATT_WIDTH), ctx_row),
            pl.BlockSpec((TM, ATT_WIDTH), ctx_row),
        ],
        out_shape=[
            jax.ShapeDtypeStruct((T_ALL, ATT_WIDTH), BF16),
            jax.ShapeDtypeStruct((T_ALL, ATT_WIDTH), BF16),
            jax.ShapeDtypeStruct((T_ALL, ATT_WIDTH), BF16),
            jax.ShapeDtypeStruct((T_ALL, MIX_IN), F32),
            jax.ShapeDtypeStruct((T_CTX, ATT_WIDTH), F32),
            jax.ShapeDtypeStruct((T_CTX, ATT_WIDTH), F32),
        ],
        compiler_params=_params(1),
    )(h, w_in)


def _softmax_pv(scores, values):
    m = functools.reduce(jnp.maximum, [jnp.max(s, axis=-1, keepdims=True) for s in scores])
    es = [jnp.exp(s - m) for s in scores]
    denom = functools.reduce(jnp.add, [jnp.sum(e, axis=-1, keepdims=True) for e in es])
    inv = 1.0 / denom
    outs = [jnp.dot((e * inv).astype(BF16), v, preferred_element_type=F32)
            for e, v in zip(es, values)]
    return functools.reduce(jnp.add, outs)


def _qk(q, k):
    return lax.dot_general(q, k, (((1,), (1,)), ((), ())), preferred_element_type=F32)


def _ctx_attn_kernel(q_ref, k_ref, v_ref, o_ref):
    for h in range(N_HEADS):
        cs = slice(h * HEAD_DIM, (h + 1) * HEAD_DIM)
        s = _qk(q_ref[:, cs], k_ref[:, cs])
        o_ref[:, cs] = _softmax_pv([s], [v_ref[:, cs]]).astype(BF16)


def _ctx_attention(q, k, v):
    blk = pl.BlockSpec((SEQ, ATT_WIDTH), lambda b: (b, 0))
    return pl.pallas_call(
        _ctx_attn_kernel,
        grid=(BATCH,),
        in_specs=[blk, blk, blk],
        out_specs=blk,
        out_shape=jax.ShapeDtypeStruct((T_ALL, D_MODEL), BF16),
        compiler_params=_params(1),
    )(q, k, v)


def _bias_kernel(ra_ref, rb_ref, o_ref):
    shape = (GRID_W, 2 * GRID_W)
    lane = lax.broadcasted_iota(jnp.int32, shape, 1)
    c = lax.broadcasted_iota(jnp.int32, shape, 0)
    kc = lane & (GRID_W - 1)
    qstart = jnp.clip(c - WIN_W // 2, 0, GRID_W - WIN_W)
    col_valid = (kc >= qstart) & (kc < qstart + WIN_W)
    first = lane < GRID_W
    neg = jnp.full(shape, NEG, F32)
    cache = {}

    def block(ri_a, ri_b):
        key = (ri_a, ri_b)
        if key not in cache:
            if ri_a is None and ri_b is None:
                cache[key] = neg
            else:
                valid = col_valid
                if ri_a is None:
                    row = rb_ref[ri_b:ri_b + 1, :]
                    valid = valid & ~first
                elif ri_b is None:
                    row = ra_ref[ri_a:ri_a + 1, :]
                    valid = valid & first
                else:
                    row = ra_ref[ri_a:ri_a + 1, :] + rb_ref[ri_b:ri_b + 1, :]
                rolled = pltpu.roll(jnp.broadcast_to(row, shape), 2 * GRID_W - (WIN_W - 1), 1,
                                    stride=1, stride_axis=0)
                cache[key] = jnp.where(valid, rolled, neg)
        return cache[key]

    for cls, (r0, k0) in enumerate(((0, 0), (Q_ROWS, 0), (GRID_H - Q_ROWS, GRID_H - K_ROWS))):
        for ql in range(Q_ROWS):
            qr = r0 + ql
            sr = min(max(qr - WIN_H // 2, 0), GRID_H - WIN_H)
            for pj in range(K_ROWS // 2):
                ris = []
                for e in range(2):
                    kr = k0 + 2 * pj + e
                    ris.append(kr - qr + WIN_H - 1 if sr <= kr < sr + WIN_H else None)
                o_ref[cls, ql * GRID_W:(ql + 1) * GRID_W,
                      pj * 2 * GRID_W:(pj + 1) * 2 * GRID_W] = block(*ris)


def _bias_tables(rpb):
    n_ri, n_ci = 2 * WIN_H - 1, 2 * WIN_W - 1
    ra = jnp.pad(rpb, ((0, 0), (0, 0), (0, 16 - n_ri), (0, 2 * GRID_W - n_ci)))
    rb = jnp.pad(rpb, ((0, 0), (0, 0), (0, 16 - n_ri), (GRID_W, GRID_W - n_ci)))
    spec = pl.BlockSpec((None, None, 16, 2 * GRID_W), lambda l, h: (l, h, 0, 0))
    return pl.pallas_call(
        _bias_kernel,
        grid=(DEPTH, N_HEADS),
        in_specs=[spec, spec],
        out_specs=pl.BlockSpec((None, None, 3, QB, KB), lambda l, h: (l, h, 0, 0, 0)),
        out_shape=jax.ShapeDtypeStruct((DEPTH, N_HEADS, 3, QB, KB), F32),
        compiler_params=_params(2),
    )(ra, rb)


def _lat_attn_kernel(q_ref, k_ref, v_ref, kc_ref, vc_ref, tab_ref, cat_ref, o_ref):
    del cat_ref
    kc = kc_ref[...].astype(BF16)
    vc = vc_ref[...].astype(BF16)
    for i in range(N_QBLK):
        cls = 0 if i == 0 else (2 if i == N_QBLK - 1 else 1)
        k0 = min(max(Q_ROWS * i - WIN_H // 2, 0), GRID_H - K_ROWS) * GRID_W
        qb = q_ref[i * QB:(i + 1) * QB, :]
        tab = tab_ref[cls]
        s_win = _qk(qb, k_ref[k0:k0 + KB, :])
        s_win = jnp.where(tab > 0.5 * NEG, s_win + tab, NEG)
        s_ctx = _qk(qb, kc)
        o = _softmax_pv([s_win, s_ctx], [v_ref[k0:k0 + KB, :], vc])
        o_ref[i * QB:(i + 1) * QB, :] = o.astype(BF16)


def _lat_attention(q, k, v, cache_k, cache_v, tables, cat, l):
    tok = pl.BlockSpec((DEC_SEQ, HEAD_DIM), lambda h, b: (T_CTX // DEC_SEQ + b, h))
    ctx = pl.BlockSpec((None, None, PAST_LEN, HEAD_DIM), lambda h, b: (b, l, 0, h))
    return pl.pallas_call(
        _lat_attn_kernel,
        grid=(N_HEADS, DEC_BATCH),
        in_specs=[
            tok, tok, tok, ctx, ctx,
            pl.BlockSpec((None, None, 3, QB, KB), lambda h, b: (l, h, 0, 0, 0)),
            pl.BlockSpec(memory_space=pl.ANY),
        ],
        out_specs=tok,
        out_shape=jax.ShapeDtypeStruct((T_ALL, D_MODEL), BF16),
        input_output_aliases={6: 0},
        compiler_params=_params(2),
    )(q, k, v, cache_k, cache_v, tables, cat)


def _mixer_kernel(prev_ref, cur_ref, next_ref, wdw_ref, bdw_ref, lng_ref, lnb_ref, wpw_ref,
                  bpw_ref, wpool_ref, pscale_ref, cat_ref, o_ref, hp_ref, up_ref, conv_ref):
    del cat_ref
    i = pl.program_id(0)
    n_ctx = T_CTX // CHUNK
    per_seq = DEC_SEQ // CHUNK
    is_ctx = i < n_ctx
    j = (i - n_ctx) % per_seq
    has_prev = jnp.logical_and(jnp.logical_not(is_ctx), j != 0)
    has_next = jnp.logical_and(jnp.logical_not(is_ctx), j != per_seq - 1)
    cw = CONV_WIDTH

    def glu(ref):
        return ref[:, 0:cw] * jax.nn.sigmoid(ref[:, cw:2 * cw])

    def pool_in(ref):
        return ref[:, 2 * cw:MIX_IN]

    zeros = jnp.zeros((HALO, cw), F32)
    hp_ref[0:HALO, :] = jnp.where(has_prev, glu(prev_ref), zeros)
    hp_ref[HALO:HALO + CHUNK, :] = glu(cur_ref)
    hp_ref[HALO + CHUNK:, :] = jnp.where(has_next, glu(next_ref), zeros)
    up_ref[0:HALO, :] = jnp.where(has_prev, pool_in(prev_ref), zeros)
    up_ref[HALO:HALO + CHUNK, :] = pool_in(cur_ref)
    up_ref[HALO + CHUNK:, :] = jnp.where(has_next, pool_in(next_ref), zeros)

    rb = 128
    for cb in range(cw // 128):
        cs = slice(cb * 128, (cb + 1) * 128)
        for r0 in range(0, CHUNK, rb):
            acc = None
            for tap in range(CONV_K):
                off = HALO - CONV_K // 2 + r0 + tap
                term = hp_ref[off:off + rb, cs] * wdw_ref[tap:tap + 1, cs]
                acc = term if acc is None else acc + term
            conv_ref[r0:r0 + rb, cs] = acc + bdw_ref[:, cs]

    hc = conv_ref[...]
    mu = jnp.mean(hc, axis=-1, keepdims=True)
    d = hc - mu
    var = jnp.mean(d * d, axis=-1, keepdims=True)
    y = _silu(d * lax.rsqrt(var + EPS) * lng_ref[...] + lnb_ref[...])
    cm = jnp.dot(y.astype(BF16), wpw_ref[...], preferred_element_type=F32) + bpw_ref[...]
    o_ref[:, 0:cw] = cm.astype(BF16)

    seq_len = jnp.where(is_ctx, SEQ, DEC_SEQ)
    pos = jnp.where(is_ctx, 0, j * CHUNK) + lax.broadcasted_iota(jnp.int32, (CHUNK, 1), 0)
    for gi, w in enumerate(POOL_WINDOWS):
        cs = slice(gi * POOL_GROUP, (gi + 1) * POOL_GROUP)
        total = None
        for dlt in range(-(w // 2), w - w // 2):
            term = up_ref[HALO + dlt:HALO + dlt + CHUNK, cs]
            total = term if total is None else total + term
        lo = jnp.clip(pos - w // 2, 0, seq_len)
        hi = jnp.clip(pos - w // 2 + w, 0, seq_len)
        diff = total / (hi - lo).astype(F32) - up_ref[HALO:HALO + CHUNK, cs]
        y = jnp.dot(diff.astype(BF16), wpool_ref[gi], preferred_element_type=F32)
        o_ref[:, cw + gi * POOL_GROUP:cw + (gi + 1) * POOL_GROUP] = (
            y * pscale_ref[:, cs]).astype(BF16)


def _mixer(u, w_dw, b_dw, ln_g, ln_b, w_pw, b_pw, w_pool, pool_scale, cat, l):
    per = CHUNK // HALO
    n_halo = T_ALL // HALO
    return pl.pallas_call(
        _mixer_kernel,
        grid=(T_ALL // CHUNK,),
        in_specs=[
            pl.BlockSpec((HALO, MIX_IN), lambda i: (jnp.maximum(i * per - 1, 0), 0)),
            pl.BlockSpec((CHUNK, MIX_IN), lambda i: (i, 0)),
            pl.BlockSpec((HALO, MIX_IN), lambda i: (jnp.minimum((i + 1) * per, n_halo - 1), 0)),
            pl.BlockSpec((None, CONV_K, CONV_WIDTH), lambda i: (l, 0, 0)),
            _layer_vec_spec(l, CONV_WIDTH),
            _layer_vec_spec(l, CONV_WIDTH),
            _layer_vec_spec(l, CONV_WIDTH),
            pl.BlockSpec((None, CONV_WIDTH, CONV_WIDTH), lambda i: (l, 0, 0)),
            _layer_vec_spec(l, CONV_WIDTH),
            pl.BlockSpec((None, len(POOL_WINDOWS), POOL_GROUP, POOL_GROUP),
                         lambda i: (l, 0, 0, 0)),
            _layer_vec_spec(l, POOL_WIDTH),
            pl.BlockSpec(memory_space=pl.ANY),
        ],
        out_specs=pl.BlockSpec((CHUNK, CONV_WIDTH + POOL_WIDTH), lambda i: (i, 1)),
        out_shape=jax.ShapeDtypeStruct((T_ALL, D_MODEL), BF16),
        scratch_shapes=[
            pltpu.VMEM((CHUNK + 2 * HALO, CONV_WIDTH), F32),
            pltpu.VMEM((CHUNK + 2 * HALO, POOL_WIDTH), F32),
            pltpu.VMEM((CHUNK, CONV_WIDTH), F32),
        ],
        input_output_aliases={11: 0},
        compiler_params=_params(1),
    )(u, u, u, w_dw, b_dw, ln_g, ln_b, w_pw, b_pw, w_pool, pool_scale, cat)


def _outproj_kernel(cat_ref, w_ref, x_ref, gpost_ref, g1_ref, gpre_ref, sh_ref, sc_ref,
                    xo_ref, ho_ref):
    o = jnp.dot(cat_ref[...], w_ref[...], preferred_element_type=F32)
    x1 = x_ref[...] + g1_ref[...] * _rms(o, gpost_ref[...])
    xo_ref[...] = x1
    ho_ref[...] = (_rms(x1, gpre_ref[...]) * (1 + sc_ref[...]) + sh_ref[...]).astype(BF16)


def _outproj(cat, w_out, x, g_post_mix, g_pre_ffn, mods, l):
    row = pl.BlockSpec((TM, D_MODEL), lambda i: (i, 0))
    return pl.pallas_call(
        _outproj_kernel,
        grid=(T_ALL // TM,),
        in_specs=[
            row,
            pl.BlockSpec((None, D_MODEL, D_MODEL), lambda i: (l, 0, 0),
                         pipeline_mode=pl.Buffered(1)),
            row,
            _layer_vec_spec(l, D_MODEL),
            _mod_spec(l, 2, TM),
            _layer_vec_spec(l, D_MODEL),
            _mod_spec(l, 3, TM),
            _mod_spec(l, 4, TM),
        ],
        out_specs=[row, row],
        out_shape=[
            jax.ShapeDtypeStruct((T_ALL, D_MODEL), F32),
            jax.ShapeDtypeStruct((T_ALL, D_MODEL), BF16),
        ],
        compiler_params=_params(1),
    )(cat, w_out, x, g_post_mix, mods, g_pre_ffn, mods, mods)


def _ffn_kernel(*refs, with_next):
    if with_next:
        (h_ref, wg_ref, wu_ref, wo_ref, x_ref, gpost_ref, g2_ref, gpre_ref, sh_ref, sc_ref,
         xo_ref, ho_ref, acc_ref) = refs
    else:
        h_ref, wg_ref, wu_ref, wo_ref, x_ref, gpost_ref, g2_ref, xo_ref, acc_ref = refs
    f = pl.program_id(1)
    h = h_ref[...]
    gt = jnp.dot(h, wg_ref[...], preferred_element_type=F32)
    up = jnp.dot(h, wu_ref[...], preferred_element_type=F32)
    act = (_silu(gt) * up).astype(BF16)
    part = jnp.dot(act, wo_ref[...], preferred_element_type=F32)

    @pl.when(f == 0)
    def _():
        acc_ref[...] = part

    @pl.when(f > 0)
    def _():
        acc_ref[...] += part

    @pl.when(f == D_FF // TF - 1)
    def _():
        x2 = x_ref[...] + g2_ref[...] * _rms(acc_ref[...], gpost_ref[...])
        xo_ref[...] = x2
        if with_next:
            ho_ref[...] = (_rms(x2, gpre_ref[...]) * (1 + sc_ref[...]) + sh_ref[...]).astype(BF16)


def _ffn(h, w_ffn_in, w_ffn_out, x, g_post_ffn, g_pre_mix, mods, l):
    with_next = l + 1 < DEPTH
    nf = D_FF // TF
    row = pl.BlockSpec((TM, D_MODEL), lambda i, f: (i, 0))
    in_specs = [
        row,
        pl.BlockSpec((None, D_MODEL, TF), lambda i, f: (l, 0, f)),
        pl.BlockSpec((None, D_MODEL, TF), lambda i, f: (l, 0, nf + f)),
        pl.BlockSpec((None, TF, D_MODEL), lambda i, f: (l, f, 0)),
        row,
        _layer_vec_spec(l, D_MODEL),
        _mod_spec(l, 5, TM),
    ]
    args = [h, w_ffn_in, w_ffn_in, w_ffn_out, x, g_post_ffn, mods]
    out_specs = [row]
    out_shape = [jax.ShapeDtypeStruct((T_ALL, D_MODEL), F32)]
    if with_next:
        in_specs += [_layer_vec_spec(l + 1, D_MODEL), _mod_spec(l + 1, 0, TM),
                     _mod_spec(l + 1, 1, TM)]
        args += [g_pre_mix, mods, mods]
        out_specs.append(row)
        out_shape.append(jax.ShapeDtypeStruct((T_ALL, D_MODEL), BF16))
    outs = pl.pallas_call(
        functools.partial(_ffn_kernel, with_next=with_next),
        grid=(T_ALL // TM, nf),
        in_specs=in_specs,
        out_specs=out_specs,
        out_shape=out_shape,
        scratch_shapes=[pltpu.VMEM((TM, D_MODEL), F32)],
        compiler_params=_params(2),
    )(*args)
    return outs if with_next else (outs[0], None)


def kernel(x_prompt, x_sample, cache_k, cache_v, c, c_ctx, w_ada, b_ada, g_pre_mix, g_post_mix,
           g_pre_ffn, g_post_ffn, w_in, rpb, w_dw, b_dw, ln_conv_g, ln_conv_b, w_pw, b_pw,
           w_pool, pool_scale, w_out, w_ffn_in, w_ffn_out):
    x = jnp.concatenate([x_prompt.reshape(T_CTX, D_MODEL), x_sample.reshape(T_LAT, D_MODEL)])
    cv = jnp.concatenate(
        [c_ctx[None, :], c, jnp.zeros((MOD_ROWS - 1 - DEC_BATCH, D_MODEL), F32)])
    mods = _modulations(cv, w_ada, b_ada)
    tables = _bias_tables(rpb)

    vec = lambda a: a.reshape(DEPTH, 1, a.shape[-1])
    g_pre_mix, g_post_mix, g_pre_ffn, g_post_ffn = map(
        vec, (g_pre_mix, g_post_mix, g_pre_ffn, g_post_ffn))
    b_dw, ln_conv_g, ln_conv_b, b_pw, pool_scale = map(
        vec, (b_dw, ln_conv_g, ln_conv_b, b_pw, pool_scale))
    w_dw = w_dw.reshape(DEPTH, CONV_K, CONV_WIDTH)
    w_in, w_out, w_ffn_in, w_ffn_out, w_pw, w_pool = (
        w.astype(BF16) for w in (w_in, w_out, w_ffn_in, w_ffn_out, w_pw, w_pool))
    cache_k = cache_k.reshape(DEC_BATCH, DEPTH, PAST_LEN, ATT_WIDTH)
    cache_v = cache_v.reshape(DEC_BATCH, DEPTH, PAST_LEN, ATT_WIDTH)

    h = _prenorm(x, g_pre_mix, mods, 0)
    new_k, new_v = [], []
    for l in range(DEPTH):
        q, k, v, u, kf, vf = _inproj(h, w_in, l)
        new_k.append(kf.reshape(BATCH, SEQ, N_HEADS, HEAD_DIM))
        new_v.append(vf.reshape(BATCH, SEQ, N_HEADS, HEAD_DIM))
        cat = _ctx_attention(q, k, v)
        cat = _lat_attention(q, k, v, cache_k, cache_v, tables, cat, l)
        cat = _mixer(u, w_dw, b_dw, ln_conv_g, ln_conv_b, w_pw, b_pw, w_pool, pool_scale, cat, l)
        x, h2 = _outproj(cat, w_out, x, g_post_mix, g_pre_ffn, mods, l)
        x, h = _ffn(h2, w_ffn_in, w_ffn_out, x, g_post_ffn, g_pre_mix, mods, l)

    y_prompt = x[:T_CTX].reshape(BATCH, SEQ, D_MODEL)
    y_sample = x[T_CTX:].reshape(DEC_BATCH, DEC_SEQ, D_MODEL)
    return (y_prompt, y_sample, jnp.stack(new_k, axis=1), jnp.stack(new_v, axis=1))
```

```python
import functools

import jax
import jax.numpy as jnp
from jax import lax
from jax.experimental import pallas as pl
from jax.experimental.pallas import tpu as pltpu

F32 = jnp.float32
BF16 = jnp.bfloat16

D_MODEL = 2048
BATCH = 32
SEQ = 256
DEPTH = 4
DEC_BATCH = 4
DEC_SEQ = 2048
PAST_LEN = 256
GRID_W = 64
GRID_H = DEC_SEQ // GRID_W
WIN_H = 8
WIN_W = 16
ATT_WIDTH = D_MODEL // 2
N_HEADS = 8
HEAD_DIM = ATT_WIDTH // N_HEADS
CONV_WIDTH = D_MODEL // 4
CONV_K = 31
POOL_WIDTH = D_MODEL - ATT_WIDTH - CONV_WIDTH
POOL_WINDOWS = (2, 4, 8, 16)
POOL_GROUP = POOL_WIDTH // len(POOL_WINDOWS)
MIX_IN = 2 * CONV_WIDTH + POOL_WIDTH
D_IN = 3 * ATT_WIDTH + MIX_IN
D_FF = 5632
N_MOD = 6
EPS = 1e-6
NEG = -1e30
Q_SCALE = HEAD_DIM ** -0.5

T_CTX = BATCH * SEQ
T_LAT = DEC_BATCH * DEC_SEQ
T_ALL = T_CTX + T_LAT
MOD_ROWS = 8

VMEM_LIMIT = 56 * 1024 * 1024

TM = 512
TF = 512
FFN_OUT_COLS = 512
OUT_ROWS = 256
MOD_TN = 1024
CHUNK = 256
HALO = 16
SUBLANES = 8

Q_ROWS = 4
K_ROWS = 12
N_QBLK = GRID_H // Q_ROWS
QB = Q_ROWS * GRID_W
KB = K_ROWS * GRID_W


def _params(n_axes):
    return pltpu.CompilerParams(
        dimension_semantics=("arbitrary",) * n_axes, vmem_limit_bytes=VMEM_LIMIT)


def _rms(x, g):
    ms = jnp.mean(x * x, axis=-1, keepdims=True)
    return x * lax.rsqrt(ms + EPS) * g


def _silu(x):
    return x * jax.nn.sigmoid(x)


def _mod_row(i, tm):
    return jnp.where(i * tm < T_CTX, 0, 1 + (i * tm - T_CTX) // DEC_SEQ)


def _mod_spec(l, chunk, tm):
    return pl.BlockSpec((None, None, None, 1, D_MODEL),
                        lambda i, *_: (l, _mod_row(i, tm), chunk, 0, 0))


def _layer_vec_spec(l, width):
    return pl.BlockSpec((None, 1, width), lambda *_: (l, 0, 0))


def _mod_kernel(cv_ref, w_ref, b_ref, o_ref):
    s = _silu(cv_ref[...]).astype(BF16)
    o_ref[...] = jnp.dot(s, w_ref[...].astype(BF16), preferred_element_type=F32) + b_ref[...]


def _modulations(cv, w_ada, b_ada):
    n = N_MOD * D_MODEL
    out = pl.pallas_call(
        _mod_kernel,
        grid=(DEPTH, n // MOD_TN),
        in_specs=[
            pl.BlockSpec((MOD_ROWS, D_MODEL), lambda l, j: (0, 0)),
            pl.BlockSpec((None, D_MODEL, MOD_TN), lambda l, j: (l, 0, j)),
            pl.BlockSpec((None, 1, MOD_TN), lambda l, j: (l, 0, j)),
        ],
        out_specs=pl.BlockSpec((None, MOD_ROWS, MOD_TN), lambda l, j: (l, 0, j)),
        out_shape=jax.ShapeDtypeStruct((DEPTH, MOD_ROWS, n), F32),
        compiler_params=_params(2),
    )(cv, w_ada, b_ada.reshape(DEPTH, 1, n))
    return out.reshape(DEPTH, MOD_ROWS, N_MOD, 1, D_MODEL)


def _split_specs(tm):
    n_ctx = T_CTX // tm
    return [pl.BlockSpec((tm, D_MODEL), lambda i, *_: (jnp.minimum(i, n_ctx - 1), 0)),
            pl.BlockSpec((tm, D_MODEL), lambda i, *_: (jnp.maximum(i - n_ctx, 0), 0))]


def _split_rows(xp_ref, xs_ref, rows, tm):
    return jnp.where(pl.program_id(0) < T_CTX // tm, xp_ref[rows, :], xs_ref[rows, :])


def _prenorm_kernel(xp_ref, xs_ref, g_ref, sh_ref, sc_ref, h_ref):
    x = _split_rows(xp_ref, xs_ref, slice(None), TM)
    h_ref[...] = (_rms(x, g_ref[...]) * (1 + sc_ref[...]) + sh_ref[...]).astype(BF16)


def _prenorm(xp, xs, g_pre, mods, l):
    return pl.pallas_call(
        _prenorm_kernel,
        grid=(T_ALL // TM,),
        in_specs=_split_specs(TM) + [
            _layer_vec_spec(l, D_MODEL),
            _mod_spec(l, 0, TM),
            _mod_spec(l, 1, TM),
        ],
        out_specs=pl.BlockSpec((TM, D_MODEL), lambda i: (i, 0)),
        out_shape=jax.ShapeDtypeStruct((T_ALL, D_MODEL), BF16),
        compiler_params=_params(1),
    )(xp, xs, g_pre, mods, mods)


def _inproj_kernel(*refs):
    h_ref, w_ref = refs[:2]
    q_ref, k_ref, v_ref, u_ref, kf_ref, vf_ref = refs[-6:]
    i = pl.program_id(0)
    h = h_ref[...]
    q = jnp.dot(h, w_ref[:, 0:ATT_WIDTH], preferred_element_type=F32)
    q_ref[...] = (q * Q_SCALE).astype(BF16)
    k = jnp.dot(h, w_ref[:, ATT_WIDTH:2 * ATT_WIDTH], preferred_element_type=F32)
    k_ref[...] = k.astype(BF16)
    v = jnp.dot(h, w_ref[:, 2 * ATT_WIDTH:3 * ATT_WIDTH], preferred_element_type=F32)
    v_ref[...] = v.astype(BF16)
    u_ref[...] = jnp.dot(h, w_ref[:, 3 * ATT_WIDTH:D_IN], preferred_element_type=F32)

    @pl.when(i < T_CTX // TM)
    def _():
        for b in range(TM // SEQ):
            kf_ref[b] = k[b * SEQ:(b + 1) * SEQ, :]
            vf_ref[b] = v[b * SEQ:(b + 1) * SEQ, :]


def _inproj(h, w_in, caches, l):
    n_ctx = T_CTX // TM
    row = lambda i: (i, 0)
    cache_spec = pl.BlockSpec((TM // SEQ, None, SEQ, ATT_WIDTH),
                              lambda i: (jnp.minimum(i, n_ctx - 1), l, 0, 0))
    cache_shape = jax.ShapeDtypeStruct((BATCH, DEPTH, SEQ, ATT_WIDTH), F32)
    in_specs = [
        pl.BlockSpec((TM, D_MODEL), row),
        pl.BlockSpec((None, D_MODEL, D_IN), lambda i: (l, 0, 0), pipeline_mode=pl.Buffered(1)),
    ]
    args = [h, w_in]
    aliases = {}
    if caches is not None:
        in_specs += [pl.BlockSpec(memory_space=pl.ANY)] * 2
        args += list(caches)
        aliases = {2: 4, 3: 5}
    return pl.pallas_call(
        _inproj_kernel,
        grid=(T_ALL // TM,),
        in_specs=in_specs,
        out_specs=[
            pl.BlockSpec((TM, ATT_WIDTH), row),
            pl.BlockSpec((TM, ATT_WIDTH), row),
            pl.BlockSpec((TM, ATT_WIDTH), row),
            pl.BlockSpec((TM, MIX_IN), row),
            cache_spec,
            cache_spec,
        ],
        out_shape=[
            jax.ShapeDtypeStruct((T_ALL, ATT_WIDTH), BF16),
            jax.ShapeDtypeStruct((T_ALL, ATT_WIDTH), BF16),
            jax.ShapeDtypeStruct((T_ALL, ATT_WIDTH), BF16),
            jax.ShapeDtypeStruct((T_ALL, MIX_IN), F32),
            cache_shape,
            cache_shape,
        ],
        input_output_aliases=aliases,
        compiler_params=_params(1),
    )(*args)


def _softmax_pv(scores, values):
    m = functools.reduce(jnp.maximum, [jnp.max(s, axis=-1, keepdims=True) for s in scores])
    es = [jnp.exp(s - m) for s in scores]
    denom = functools.reduce(jnp.add, [jnp.sum(e, axis=-1, keepdims=True) for e in es])
    inv = 1.0 / denom
    outs = [jnp.dot((e * inv).astype(BF16), v, preferred_element_type=F32)
            for e, v in zip(es, values)]
    return functools.reduce(jnp.add, outs)


def _qk(q, k):
    return lax.dot_general(q, k, (((1,), (1,)), ((), ())), preferred_element_type=F32)


def _ctx_attn_kernel(q_ref, k_ref, v_ref, o_ref):
    for h in range(N_HEADS):
        cs = slice(h * HEAD_DIM, (h + 1) * HEAD_DIM)
        s = _qk(q_ref[:, cs], k_ref[:, cs])
        o_ref[:, cs] = _softmax_pv([s], [v_ref[:, cs]]).astype(BF16)


def _ctx_attention(q, k, v):
    blk = pl.BlockSpec((SEQ, ATT_WIDTH), lambda b: (b, 0))
    return pl.pallas_call(
        _ctx_attn_kernel,
        grid=(BATCH,),
        in_specs=[blk, blk, blk],
        out_specs=blk,
        out_shape=jax.ShapeDtypeStruct((T_ALL, D_MODEL), BF16),
        compiler_params=_params(1),
    )(q, k, v)


def _bias_kernel(ra_ref, rb_ref, o_ref):
    shape = (GRID_W, 2 * GRID_W)
    lane = lax.broadcasted_iota(jnp.int32, shape, 1)
    c = lax.broadcasted_iota(jnp.int32, shape, 0)
    kc = lane & (GRID_W - 1)
    qstart = jnp.clip(c - WIN_W // 2, 0, GRID_W - WIN_W)
    col_valid = (kc >= qstart) & (kc < qstart + WIN_W)
    first = lane < GRID_W
    neg = jnp.full(shape, NEG, F32)
    cache = {}

    def block(ri_a, ri_b):
        key = (ri_a, ri_b)
        if key not in cache:
            if ri_a is None and ri_b is None:
                cache[key] = neg
            else:
                valid = col_valid
                if ri_a is None:
                    row = rb_ref[ri_b:ri_b + 1, :]
                    valid = valid & ~first
                elif ri_b is None:
                    row = ra_ref[ri_a:ri_a + 1, :]
                    valid = valid & first
                else:
                    row = ra_ref[ri_a:ri_a + 1, :] + rb_ref[ri_b:ri_b + 1, :]
                rolled = pltpu.roll(jnp.broadcast_to(row, shape), 2 * GRID_W - (WIN_W - 1), 1,
                                    stride=1, stride_axis=0)
                cache[key] = jnp.where(valid, rolled, neg)
        return cache[key]

    for cls, (r0, k0) in enumerate(((0, 0), (Q_ROWS, 0), (GRID_H - Q_ROWS, GRID_H - K_ROWS))):
        for ql in range(Q_ROWS):
            qr = r0 + ql
            sr = min(max(qr - WIN_H // 2, 0), GRID_H - WIN_H)
            for pj in range(K_ROWS // 2):
                ris = []
                for e in range(2):
                    kr = k0 + 2 * pj + e
                    ris.append(kr - qr + WIN_H - 1 if sr <= kr < sr + WIN_H else None)
                o_ref[cls, ql * GRID_W:(ql + 1) * GRID_W,
                      pj * 2 * GRID_W:(pj + 1) * 2 * GRID_W] = block(*ris)


def _bias_tables(rpb):
    n_ri, n_ci = 2 * WIN_H - 1, 2 * WIN_W - 1
    ra = jnp.pad(rpb, ((0, 0), (0, 0), (0, 16 - n_ri), (0, 2 * GRID_W - n_ci)))
    rb = jnp.pad(rpb, ((0, 0), (0, 0), (0, 16 - n_ri), (GRID_W, GRID_W - n_ci)))
    spec = pl.BlockSpec((None, None, 16, 2 * GRID_W), lambda l, h: (l, h, 0, 0))
    return pl.pallas_call(
        _bias_kernel,
        grid=(DEPTH, N_HEADS),
        in_specs=[spec, spec],
        out_specs=pl.BlockSpec((None, None, 3, QB, KB), lambda l, h: (l, h, 0, 0, 0)),
        out_shape=jax.ShapeDtypeStruct((DEPTH, N_HEADS, 3, QB, KB), F32),
        compiler_params=_params(2),
    )(ra, rb)


def _lat_attn_kernel(q_ref, k_ref, v_ref, kc_ref, vc_ref, tab_ref, cat_ref, o_ref):
    del cat_ref
    kc = kc_ref[...].astype(BF16)
    vc = vc_ref[...].astype(BF16)
    for i in range(N_QBLK):
        cls = 0 if i == 0 else (2 if i == N_QBLK - 1 else 1)
        k0 = min(max(Q_ROWS * i - WIN_H // 2, 0), GRID_H - K_ROWS) * GRID_W
        qb = q_ref[i * QB:(i + 1) * QB, :]
        tab = tab_ref[cls]
        s_win = _qk(qb, k_ref[k0:k0 + KB, :])
        s_win = jnp.where(tab > 0.5 * NEG, s_win + tab, NEG)
        s_ctx = _qk(qb, kc)
        o = _softmax_pv([s_win, s_ctx], [v_ref[k0:k0 + KB, :], vc])
        o_ref[i * QB:(i + 1) * QB, :] = o.astype(BF16)


def _lat_attention(q, k, v, cache_k, cache_v, tables, cat, l):
    tok = pl.BlockSpec((DEC_SEQ, HEAD_DIM), lambda h, b: (T_CTX // DEC_SEQ + b, h))
    ctx = pl.BlockSpec((None, None, PAST_LEN, HEAD_DIM), lambda h, b: (b, l, 0, h))
    return pl.pallas_call(
        _lat_attn_kernel,
        grid=(N_HEADS, DEC_BATCH),
        in_specs=[
            tok, tok, tok, ctx, ctx,
            pl.BlockSpec((None, None, 3, QB, KB), lambda h, b: (l, h, 0, 0, 0)),
            pl.BlockSpec(memory_space=pl.ANY),
        ],
        out_specs=tok,
        out_shape=jax.ShapeDtypeStruct((T_ALL, D_MODEL), BF16),
        input_output_aliases={6: 0},
        compiler_params=_params(2),
    )(q, k, v, cache_k, cache_v, tables, cat)


def _mixer_kernel(prev_ref, cur_ref, next_ref, wdw_ref, bdw_ref, lng_ref, lnb_ref, wpw_ref,
                  bpw_ref, wpool_ref, pscale_ref, cat_ref, o_ref, hp_ref, up_ref, conv_ref):
    del cat_ref
    i = pl.program_id(0)
    n_ctx = T_CTX // CHUNK
    per_seq = DEC_SEQ // CHUNK
    is_ctx = i < n_ctx
    j = (i - n_ctx) % per_seq
    has_prev = jnp.logical_and(jnp.logical_not(is_ctx), j != 0)
    has_next = jnp.logical_and(jnp.logical_not(is_ctx), j != per_seq - 1)
    cw = CONV_WIDTH

    def glu(ref):
        return ref[:, 0:cw] * jax.nn.sigmoid(ref[:, cw:2 * cw])

    def pool_in(ref):
        return ref[:, 2 * cw:MIX_IN]

    zeros = jnp.zeros((HALO, cw), F32)
    hp_ref[0, 0:HALO, :] = jnp.where(has_prev, glu(prev_ref), zeros)
    hp_ref[0, HALO:HALO + CHUNK, :] = glu(cur_ref)
    hp_ref[0, HALO + CHUNK:, :] = jnp.where(has_next, glu(next_ref), zeros)
    up_ref[0:HALO, :] = jnp.where(has_prev, pool_in(prev_ref), zeros)
    up_ref[HALO:HALO + CHUNK, :] = pool_in(cur_ref)
    up_ref[HALO + CHUNK:, :] = jnp.where(has_next, pool_in(next_ref), zeros)

    n_shift = CHUNK + 2 * HALO - SUBLANES
    for s in range(1, SUBLANES):
        hp_ref[s, 0:n_shift, :] = hp_ref[0, s:s + n_shift, :]

    rb = 128
    for cb in range(cw // 128):
        cs = slice(cb * 128, (cb + 1) * 128)
        for r0 in range(0, CHUNK, rb):
            acc = None
            for tap in range(CONV_K):
                off = HALO - CONV_K // 2 + r0 + tap
                base = off - off % SUBLANES
                term = hp_ref[off % SUBLANES, base:base + rb, cs] * wdw_ref[tap:tap + 1, cs]
                acc = term if acc is None else acc + term
            conv_ref[r0:r0 + rb, cs] = acc + bdw_ref[:, cs]

    hc = conv_ref[...]
    mu = jnp.mean(hc, axis=-1, keepdims=True)
    d = hc - mu
    var = jnp.mean(d * d, axis=-1, keepdims=True)
    y = _silu(d * lax.rsqrt(var + EPS) * lng_ref[...] + lnb_ref[...])
    cm = jnp.dot(y.astype(BF16), wpw_ref[...], preferred_element_type=F32) + bpw_ref[...]
    o_ref[:, 0:cw] = cm.astype(BF16)

    seq_len = jnp.where(is_ctx, SEQ, DEC_SEQ)
    pos = jnp.where(is_ctx, 0, j * CHUNK) + lax.broadcasted_iota(jnp.int32, (CHUNK, 1), 0)
    for gi, w in enumerate(POOL_WINDOWS):
        cs = slice(gi * POOL_GROUP, (gi + 1) * POOL_GROUP)
        total = None
        for dlt in range(-(w // 2), w - w // 2):
            term = up_ref[HALO + dlt:HALO + dlt + CHUNK, cs]
            total = term if total is None else total + term
        lo = jnp.clip(pos - w // 2, 0, seq_len)
        hi = jnp.clip(pos - w // 2 + w, 0, seq_len)
        diff = total / (hi - lo).astype(F32) - up_ref[HALO:HALO + CHUNK, cs]
        y = jnp.dot(diff.astype(BF16), wpool_ref[gi], preferred_element_type=F32)
        o_ref[:, cw + gi * POOL_GROUP:cw + (gi + 1) * POOL_GROUP] = (
            y * pscale_ref[:, cs]).astype(BF16)


def _mixer(u, w_dw, b_dw, ln_g, ln_b, w_pw, b_pw, w_pool, pool_scale, cat, l):
    per = CHUNK // HALO
    n_halo = T_ALL // HALO
    return pl.pallas_call(
        _mixer_kernel,
        grid=(T_ALL // CHUNK,),
        in_specs=[
            pl.BlockSpec((HALO, MIX_IN), lambda i: (jnp.maximum(i * per - 1, 0), 0)),
            pl.BlockSpec((CHUNK, MIX_IN), lambda i: (i, 0)),
            pl.BlockSpec((HALO, MIX_IN), lambda i: (jnp.minimum((i + 1) * per, n_halo - 1), 0)),
            pl.BlockSpec((None, CONV_K, CONV_WIDTH), lambda i: (l, 0, 0)),
            _layer_vec_spec(l, CONV_WIDTH),
            _layer_vec_spec(l, CONV_WIDTH),
            _layer_vec_spec(l, CONV_WIDTH),
            pl.BlockSpec((None, CONV_WIDTH, CONV_WIDTH), lambda i: (l, 0, 0)),
            _layer_vec_spec(l, CONV_WIDTH),
            pl.BlockSpec((None, len(POOL_WINDOWS), POOL_GROUP, POOL_GROUP),
                         lambda i: (l, 0, 0, 0)),
            _layer_vec_spec(l, POOL_WIDTH),
            pl.BlockSpec(memory_space=pl.ANY),
        ],
        out_specs=pl.BlockSpec((CHUNK, CONV_WIDTH + POOL_WIDTH), lambda i: (i, 1)),
        out_shape=jax.ShapeDtypeStruct((T_ALL, D_MODEL), BF16),
        scratch_shapes=[
            pltpu.VMEM((SUBLANES, CHUNK + 2 * HALO, CONV_WIDTH), F32),
            pltpu.VMEM((CHUNK + 2 * HALO, POOL_WIDTH), F32),
            pltpu.VMEM((CHUNK, CONV_WIDTH), F32),
        ],
        input_output_aliases={11: 0},
        compiler_params=_params(1),
    )(u, u, u, w_dw, b_dw, ln_g, ln_b, w_pw, b_pw, w_pool, pool_scale, cat)


def _outproj_kernel(*refs, split):
    cat_ref, w_ref = refs[:2]
    gpost_ref, g1_ref, gpre_ref, sh_ref, sc_ref, xo_ref, ho_ref = refs[-7:]
    for r0 in range(0, TM, OUT_ROWS):
        rows = slice(r0, r0 + OUT_ROWS)
        x = _split_rows(refs[2], refs[3], rows, TM) if split else refs[2][rows, :]
        o = jnp.dot(cat_ref[rows, :], w_ref[...], preferred_element_type=F32)
        x1 = x + g1_ref[...] * _rms(o, gpost_ref[...])
        xo_ref[rows, :] = x1
        ho_ref[rows, :] = (
            _rms(x1, gpre_ref[...]) * (1 + sc_ref[...]) + sh_ref[...]).astype(BF16)


def _outproj(cat, w_out, xs, g_post_mix, g_pre_ffn, mods, l):
    row = pl.BlockSpec((TM, D_MODEL), lambda i: (i, 0))
    split = len(xs) == 2
    return pl.pallas_call(
        functools.partial(_outproj_kernel, split=split),
        grid=(T_ALL // TM,),
        in_specs=[
            row,
            pl.BlockSpec((None, D_MODEL, D_MODEL), lambda i: (l, 0, 0),
                         pipeline_mode=pl.Buffered(1)),
        ] + (_split_specs(TM) if split else [row]) + [
            _layer_vec_spec(l, D_MODEL),
            _mod_spec(l, 2, TM),
            _layer_vec_spec(l, D_MODEL),
            _mod_spec(l, 3, TM),
            _mod_spec(l, 4, TM),
        ],
        out_specs=[row, row],
        out_shape=[
            jax.ShapeDtypeStruct((T_ALL, D_MODEL), F32),
            jax.ShapeDtypeStruct((T_ALL, D_MODEL), BF16),
        ],
        compiler_params=_params(1),
    )(cat, w_out, *xs, g_post_mix, mods, g_pre_ffn, mods, mods)


def _ffn_kernel(*refs, with_next):
    if with_next:
        (h_ref, wg_ref, wu_ref, wo_ref, x_ref, gpost_ref, g2_ref, gpre_ref, sh_ref, sc_ref,
         xo_ref, ho_ref, acc_ref) = refs
    else:
        h_ref, wg_ref, wu_ref, wo_ref, x_ref, gpost_ref, g2_ref, xo_ref, acc_ref = refs
    f = pl.program_id(1)

    @pl.when(f == 0)
    def _():
        acc_ref[...] = jnp.zeros_like(acc_ref)

    h = h_ref[...]
    gt = jnp.dot(h, wg_ref[...], preferred_element_type=F32)
    up = jnp.dot(h, wu_ref[...], preferred_element_type=F32)
    act = (_silu(gt) * up).astype(BF16)
    for c0 in range(0, D_MODEL, FFN_OUT_COLS):
        cols = slice(c0, c0 + FFN_OUT_COLS)
        acc_ref[:, cols] += jnp.dot(act, wo_ref[:, cols], preferred_element_type=F32)

    @pl.when(f == D_FF // TF - 1)
    def _():
        x2 = x_ref[...] + g2_ref[...] * _rms(acc_ref[...], gpost_ref[...])
        xo_ref[...] = x2
        if with_next:
            ho_ref[...] = (_rms(x2, gpre_ref[...]) * (1 + sc_ref[...]) + sh_ref[...]).astype(BF16)


def _ffn(h, w_ffn_in, w_ffn_out, x, g_post_ffn, g_pre_mix, mods, l):
    with_next = l + 1 < DEPTH
    nf = D_FF // TF
    row = pl.BlockSpec((TM, D_MODEL), lambda i, f: (i, 0))
    in_specs = [
        row,
        pl.BlockSpec((None, D_MODEL, TF), lambda i, f: (l, 0, f)),
        pl.BlockSpec((None, D_MODEL, TF), lambda i, f: (l, 0, nf + f)),
        pl.BlockSpec((None, TF, D_MODEL), lambda i, f: (l, f, 0)),
        row,
        _layer_vec_spec(l, D_MODEL),
        _mod_spec(l, 5, TM),
    ]
    args = [h, w_ffn_in, w_ffn_in, w_ffn_out, x, g_post_ffn, mods]
    out_specs = [row]
    out_shape = [jax.ShapeDtypeStruct((T_ALL, D_MODEL), F32)]
    if with_next:
        in_specs += [_layer_vec_spec(l + 1, D_MODEL), _mod_spec(l + 1, 0, TM),
                     _mod_spec(l + 1, 1, TM)]
        args += [g_pre_mix, mods, mods]
        out_specs.append(row)
        out_shape.append(jax.ShapeDtypeStruct((T_ALL, D_MODEL), BF16))
    outs = pl.pallas_call(
        functools.partial(_ffn_kernel, with_next=with_next),
        grid=(T_ALL // TM, nf),
        in_specs=in_specs,
        out_specs=out_specs,
        out_shape=out_shape,
        scratch_shapes=[pltpu.VMEM((TM, D_MODEL), F32)],
        compiler_params=_params(2),
    )(*args)
    return outs if with_next else (outs[0], None)


def kernel(x_prompt, x_sample, cache_k, cache_v, c, c_ctx, w_ada, b_ada, g_pre_mix, g_post_mix,
           g_pre_ffn, g_post_ffn, w_in, rpb, w_dw, b_dw, ln_conv_g, ln_conv_b, w_pw, b_pw,
           w_pool, pool_scale, w_out, w_ffn_in, w_ffn_out):
    xs = (x_prompt.reshape(T_CTX, D_MODEL), x_sample.reshape(T_LAT, D_MODEL))
    cv = jnp.concatenate(
        [c_ctx[None, :], c, jnp.zeros((MOD_ROWS - 1 - DEC_BATCH, D_MODEL), F32)])
    mods = _modulations(cv, w_ada, b_ada)
    tables = _bias_tables(rpb)

    vec = lambda a: a.reshape(DEPTH, 1, a.shape[-1])
    g_pre_mix, g_post_mix, g_pre_ffn, g_post_ffn = map(
        vec, (g_pre_mix, g_post_mix, g_pre_ffn, g_post_ffn))
    b_dw, ln_conv_g, ln_conv_b, b_pw, pool_scale = map(
        vec, (b_dw, ln_conv_g, ln_conv_b, b_pw, pool_scale))
    w_dw = w_dw.reshape(DEPTH, CONV_K, CONV_WIDTH)
    w_in, w_out, w_ffn_in, w_ffn_out, w_pw, w_pool = (
        w.astype(BF16) for w in (w_in, w_out, w_ffn_in, w_ffn_out, w_pw, w_pool))
    cache_k = cache_k.reshape(DEC_BATCH, DEPTH, PAST_LEN, ATT_WIDTH)
    cache_v = cache_v.reshape(DEC_BATCH, DEPTH, PAST_LEN, ATT_WIDTH)

    h = _prenorm(*xs, g_pre_mix, mods, 0)
    caches = None
    for l in range(DEPTH):
        q, k, v, u, *caches = _inproj(h, w_in, caches, l)
        cat = _ctx_attention(q, k, v)
        cat = _lat_attention(q, k, v, cache_k, cache_v, tables, cat, l)
        cat = _mixer(u, w_dw, b_dw, ln_conv_g, ln_conv_b, w_pw, b_pw, w_pool, pool_scale, cat, l)
        x, h2 = _outproj(cat, w_out, xs, g_post_mix, g_pre_ffn, mods, l)
        x, h = _ffn(h2, w_ffn_in, w_ffn_out, x, g_post_ffn, g_pre_mix, mods, l)
        xs = (x,)

    y_prompt = x[:T_CTX].reshape(BATCH, SEQ, D_MODEL)
    y_sample = x[T_CTX:].reshape(DEC_BATCH, DEC_SEQ, D_MODEL)
    new_k, new_v = (a.reshape(BATCH, DEPTH, SEQ, N_HEADS, HEAD_DIM) for a in caches)
    return (y_prompt, y_sample, new_k, new_v)
```

```python
import functools

import jax
import jax.numpy as jnp
from jax import lax
from jax.experimental import pallas as pl
from jax.experimental.pallas import tpu as pltpu

F32 = jnp.float32
BF16 = jnp.bfloat16

D_MODEL = 2048
BATCH = 32
SEQ = 256
DEPTH = 4
DEC_BATCH = 4
DEC_SEQ = 2048
PAST_LEN = 256
GRID_W = 64
GRID_H = DEC_SEQ // GRID_W
WIN_H = 8
WIN_W = 16
ATT_WIDTH = D_MODEL // 2
N_HEADS = 8
HEAD_DIM = ATT_WIDTH // N_HEADS
CONV_WIDTH = D_MODEL // 4
CONV_K = 31
POOL_WIDTH = D_MODEL - ATT_WIDTH - CONV_WIDTH
POOL_WINDOWS = (2, 4, 8, 16)
POOL_GROUP = POOL_WIDTH // len(POOL_WINDOWS)
MIX_IN = 2 * CONV_WIDTH + POOL_WIDTH
D_IN = 3 * ATT_WIDTH + MIX_IN
D_FF = 5632
N_MOD = 6
EPS = 1e-6
NEG = -1e30
Q_SCALE = HEAD_DIM ** -0.5

T_CTX = BATCH * SEQ
T_LAT = DEC_BATCH * DEC_SEQ
T_ALL = T_CTX + T_LAT
MOD_ROWS = 8

VMEM_LIMIT = 56 * 1024 * 1024

TM = 512
TF = 512
FFN_OUT_COLS = 512
EPI_ROWS = 64
MOD_TN = 1024
CHUNK = 256
HALO = 16
SUBLANES = 8

Q_ROWS = 4
K_ROWS = 12
N_QBLK = GRID_H // Q_ROWS
QB = Q_ROWS * GRID_W
KB = K_ROWS * GRID_W


def _params(n_axes):
    return pltpu.CompilerParams(
        dimension_semantics=("arbitrary",) * n_axes, vmem_limit_bytes=VMEM_LIMIT)


def _rms(x, g):
    ms = jnp.mean(x * x, axis=-1, keepdims=True)
    return x * lax.rsqrt(ms + EPS) * g


def _silu(x):
    return x * jax.nn.sigmoid(x)


def _mod_row(i, tm):
    return jnp.where(i * tm < T_CTX, 0, 1 + (i * tm - T_CTX) // DEC_SEQ)


def _mod_spec(l, chunk, tm, tile=lambda i: i):
    return pl.BlockSpec((None, None, None, 1, D_MODEL),
                        lambda i, *_: (l, _mod_row(tile(i), tm), chunk, 0, 0))


def _layer_vec_spec(l, width):
    return pl.BlockSpec((None, 1, width), lambda *_: (l, 0, 0))


def _mod_kernel(cv_ref, w_ref, b_ref, o_ref):
    s = _silu(cv_ref[...]).astype(BF16)
    o_ref[...] = jnp.dot(s, w_ref[...].astype(BF16), preferred_element_type=F32) + b_ref[...]


def _modulations(cv, w_ada, b_ada):
    n = N_MOD * D_MODEL
    out = pl.pallas_call(
        _mod_kernel,
        grid=(DEPTH, n // MOD_TN),
        in_specs=[
            pl.BlockSpec((MOD_ROWS, D_MODEL), lambda l, j: (0, 0)),
            pl.BlockSpec((None, D_MODEL, MOD_TN), lambda l, j: (l, 0, j)),
            pl.BlockSpec((None, 1, MOD_TN), lambda l, j: (l, 0, j)),
        ],
        out_specs=pl.BlockSpec((None, MOD_ROWS, MOD_TN), lambda l, j: (l, 0, j)),
        out_shape=jax.ShapeDtypeStruct((DEPTH, MOD_ROWS, n), F32),
        compiler_params=_params(2),
    )(cv, w_ada, b_ada.reshape(DEPTH, 1, n))
    return out.reshape(DEPTH, MOD_ROWS, N_MOD, 1, D_MODEL)


def _split_specs(tm, tile=lambda i: i):
    n_ctx = T_CTX // tm
    return [pl.BlockSpec((tm, D_MODEL), lambda i, *_: (jnp.minimum(tile(i), n_ctx - 1), 0)),
            pl.BlockSpec((tm, D_MODEL), lambda i, *_: (jnp.maximum(tile(i) - n_ctx, 0), 0))]


def _split_tile(xp_ref, xs_ref, tile, tm):
    return jnp.where(tile < T_CTX // tm, xp_ref[...], xs_ref[...])


def _prenorm_kernel(xp_ref, xs_ref, g_ref, sh_ref, sc_ref, h_ref):
    x = _split_tile(xp_ref, xs_ref, pl.program_id(0), TM)
    h_ref[...] = (_rms(x, g_ref[...]) * (1 + sc_ref[...]) + sh_ref[...]).astype(BF16)


def _prenorm(xp, xs, g_pre, mods, l):
    return pl.pallas_call(
        _prenorm_kernel,
        grid=(T_ALL // TM,),
        in_specs=_split_specs(TM) + [
            _layer_vec_spec(l, D_MODEL),
            _mod_spec(l, 0, TM),
            _mod_spec(l, 1, TM),
        ],
        out_specs=pl.BlockSpec((TM, D_MODEL), lambda i: (i, 0)),
        out_shape=jax.ShapeDtypeStruct((T_ALL, D_MODEL), BF16),
        compiler_params=_params(1),
    )(xp, xs, g_pre, mods, mods)


def _inproj_kernel(*refs):
    h_ref, w_ref = refs[:2]
    q_ref, k_ref, v_ref, u_ref, kf_ref, vf_ref = refs[-6:]
    i = pl.program_id(0)
    h = h_ref[...]
    q = jnp.dot(h, w_ref[:, 0:ATT_WIDTH], preferred_element_type=F32)
    q_ref[...] = (q * Q_SCALE).astype(BF16)
    k = jnp.dot(h, w_ref[:, ATT_WIDTH:2 * ATT_WIDTH], preferred_element_type=F32)
    k_ref[...] = k.astype(BF16)
    v = jnp.dot(h, w_ref[:, 2 * ATT_WIDTH:3 * ATT_WIDTH], preferred_element_type=F32)
    v_ref[...] = v.astype(BF16)
    u_ref[...] = jnp.dot(h, w_ref[:, 3 * ATT_WIDTH:D_IN], preferred_element_type=F32)

    @pl.when(i < T_CTX // TM)
    def _():
        for b in range(TM // SEQ):
            kf_ref[b] = k[b * SEQ:(b + 1) * SEQ, :]
            vf_ref[b] = v[b * SEQ:(b + 1) * SEQ, :]


def _inproj(h, w_in, caches, l):
    n_ctx = T_CTX // TM
    row = lambda i: (i, 0)
    cache_spec = pl.BlockSpec((TM // SEQ, None, SEQ, ATT_WIDTH),
                              lambda i: (jnp.minimum(i, n_ctx - 1), l, 0, 0))
    cache_shape = jax.ShapeDtypeStruct((BATCH, DEPTH, SEQ, ATT_WIDTH), F32)
    in_specs = [
        pl.BlockSpec((TM, D_MODEL), row),
        pl.BlockSpec((None, D_MODEL, D_IN), lambda i: (l, 0, 0), pipeline_mode=pl.Buffered(1)),
    ]
    args = [h, w_in]
    aliases = {}
    if caches is not None:
        in_specs += [pl.BlockSpec(memory_space=pl.ANY)] * 2
        args += list(caches)
        aliases = {2: 4, 3: 5}
    return pl.pallas_call(
        _inproj_kernel,
        grid=(T_ALL // TM,),
        in_specs=in_specs,
        out_specs=[
            pl.BlockSpec((TM, ATT_WIDTH), row),
            pl.BlockSpec((TM, ATT_WIDTH), row),
            pl.BlockSpec((TM, ATT_WIDTH), row),
            pl.BlockSpec((TM, MIX_IN), row),
            cache_spec,
            cache_spec,
        ],
        out_shape=[
            jax.ShapeDtypeStruct((T_ALL, ATT_WIDTH), BF16),
            jax.ShapeDtypeStruct((T_ALL, ATT_WIDTH), BF16),
            jax.ShapeDtypeStruct((T_ALL, ATT_WIDTH), BF16),
            jax.ShapeDtypeStruct((T_ALL, MIX_IN), F32),
            cache_shape,
            cache_shape,
        ],
        input_output_aliases=aliases,
        compiler_params=_params(1),
    )(*args)


def _softmax_pv(scores, values):
    m = functools.reduce(jnp.maximum, [jnp.max(s, axis=-1, keepdims=True) for s in scores])
    es = [jnp.exp(s - m) for s in scores]
    denom = functools.reduce(jnp.add, [jnp.sum(e, axis=-1, keepdims=True) for e in es])
    inv = 1.0 / denom
    outs = [jnp.dot((e * inv).astype(BF16), v, preferred_element_type=F32)
            for e, v in zip(es, values)]
    return functools.reduce(jnp.add, outs)


def _qk(q, k):
    return lax.dot_general(q, k, (((1,), (1,)), ((), ())), preferred_element_type=F32)


def _ctx_attn_kernel(q_ref, k_ref, v_ref, o_ref):
    for h in range(N_HEADS):
        cs = slice(h * HEAD_DIM, (h + 1) * HEAD_DIM)
        s = _qk(q_ref[:, cs], k_ref[:, cs])
        o_ref[:, cs] = _softmax_pv([s], [v_ref[:, cs]]).astype(BF16)


def _ctx_attention(q, k, v):
    blk = pl.BlockSpec((SEQ, ATT_WIDTH), lambda b: (b, 0))
    return pl.pallas_call(
        _ctx_attn_kernel,
        grid=(BATCH,),
        in_specs=[blk, blk, blk],
        out_specs=blk,
        out_shape=jax.ShapeDtypeStruct((T_ALL, D_MODEL), BF16),
        compiler_params=_params(1),
    )(q, k, v)


def _bias_kernel(ra_ref, rb_ref, o_ref):
    shape = (GRID_W, 2 * GRID_W)
    lane = lax.broadcasted_iota(jnp.int32, shape, 1)
    c = lax.broadcasted_iota(jnp.int32, shape, 0)
    kc = lane & (GRID_W - 1)
    qstart = jnp.clip(c - WIN_W // 2, 0, GRID_W - WIN_W)
    col_valid = (kc >= qstart) & (kc < qstart + WIN_W)
    first = lane < GRID_W
    neg = jnp.full(shape, NEG, F32)
    cache = {}

    def block(ri_a, ri_b):
        key = (ri_a, ri_b)
        if key not in cache:
            if ri_a is None and ri_b is None:
                cache[key] = neg
            else:
                valid = col_valid
                if ri_a is None:
                    row = rb_ref[ri_b:ri_b + 1, :]
                    valid = valid & ~first
                elif ri_b is None:
                    row = ra_ref[ri_a:ri_a + 1, :]
                    valid = valid & first
                else:
                    row = ra_ref[ri_a:ri_a + 1, :] + rb_ref[ri_b:ri_b + 1, :]
                rolled = pltpu.roll(jnp.broadcast_to(row, shape), 2 * GRID_W - (WIN_W - 1), 1,
                                    stride=1, stride_axis=0)
                cache[key] = jnp.where(valid, rolled, neg)
        return cache[key]

    for cls, (r0, k0) in enumerate(((0, 0), (Q_ROWS, 0), (GRID_H - Q_ROWS, GRID_H - K_ROWS))):
        for ql in range(Q_ROWS):
            qr = r0 + ql
            sr = min(max(qr - WIN_H // 2, 0), GRID_H - WIN_H)
            for pj in range(K_ROWS // 2):
                ris = []
                for e in range(2):
                    kr = k0 + 2 * pj + e
                    ris.append(kr - qr + WIN_H - 1 if sr <= kr < sr + WIN_H else None)
                o_ref[cls, ql * GRID_W:(ql + 1) * GRID_W,
                      pj * 2 * GRID_W:(pj + 1) * 2 * GRID_W] = block(*ris)


def _bias_tables(rpb):
    n_ri, n_ci = 2 * WIN_H - 1, 2 * WIN_W - 1
    ra = jnp.pad(rpb, ((0, 0), (0, 0), (0, 16 - n_ri), (0, 2 * GRID_W - n_ci)))
    rb = jnp.pad(rpb, ((0, 0), (0, 0), (0, 16 - n_ri), (GRID_W, GRID_W - n_ci)))
    spec = pl.BlockSpec((None, None, 16, 2 * GRID_W), lambda l, h: (l, h, 0, 0))
    return pl.pallas_call(
        _bias_kernel,
        grid=(DEPTH, N_HEADS),
        in_specs=[spec, spec],
        out_specs=pl.BlockSpec((None, None, 3, QB, KB), lambda l, h: (l, h, 0, 0, 0)),
        out_shape=jax.ShapeDtypeStruct((DEPTH, N_HEADS, 3, QB, KB), F32),
        compiler_params=_params(2),
    )(ra, rb)


def _lat_attn_kernel(q_ref, k_ref, v_ref, kc_ref, vc_ref, tab_ref, cat_ref, o_ref):
    del cat_ref
    kc = kc_ref[...].astype(BF16)
    vc = vc_ref[...].astype(BF16)
    for i in range(N_QBLK):
        cls = 0 if i == 0 else (2 if i == N_QBLK - 1 else 1)
        k0 = min(max(Q_ROWS * i - WIN_H // 2, 0), GRID_H - K_ROWS) * GRID_W
        qb = q_ref[i * QB:(i + 1) * QB, :]
        tab = tab_ref[cls]
        s_win = _qk(qb, k_ref[k0:k0 + KB, :])
        s_win = jnp.where(tab > 0.5 * NEG, s_win + tab, NEG)
        s_ctx = _qk(qb, kc)
        o = _softmax_pv([s_win, s_ctx], [v_ref[k0:k0 + KB, :], vc])
        o_ref[i * QB:(i + 1) * QB, :] = o.astype(BF16)


def _lat_attention(q, k, v, cache_k, cache_v, tables, cat, l):
    tok = pl.BlockSpec((DEC_SEQ, HEAD_DIM), lambda h, b: (T_CTX // DEC_SEQ + b, h))
    ctx = pl.BlockSpec((None, None, PAST_LEN, HEAD_DIM), lambda h, b: (b, l, 0, h))
    return pl.pallas_call(
        _lat_attn_kernel,
        grid=(N_HEADS, DEC_BATCH),
        in_specs=[
            tok, tok, tok, ctx, ctx,
            pl.BlockSpec((None, None, 3, QB, KB), lambda h, b: (l, h, 0, 0, 0)),
            pl.BlockSpec(memory_space=pl.ANY),
        ],
        out_specs=tok,
        out_shape=jax.ShapeDtypeStruct((T_ALL, D_MODEL), BF16),
        input_output_aliases={6: 0},
        compiler_params=_params(2),
    )(q, k, v, cache_k, cache_v, tables, cat)


def _mixer_kernel(prev_ref, cur_ref, next_ref, wdw_ref, bdw_ref, lng_ref, lnb_ref, wpw_ref,
                  bpw_ref, wpool_ref, pscale_ref, cat_ref, o_ref, hp_ref, up_ref, conv_ref):
    del cat_ref
    i = pl.program_id(0)
    n_ctx = T_CTX // CHUNK
    per_seq = DEC_SEQ // CHUNK
    is_ctx = i < n_ctx
    j = (i - n_ctx) % per_seq
    has_prev = jnp.logical_and(jnp.logical_not(is_ctx), j != 0)
    has_next = jnp.logical_and(jnp.logical_not(is_ctx), j != per_seq - 1)
    cw = CONV_WIDTH

    def glu(ref):
        return ref[:, 0:cw] * jax.nn.sigmoid(ref[:, cw:2 * cw])

    def pool_in(ref):
        return ref[:, 2 * cw:MIX_IN]

    zeros = jnp.zeros((HALO, cw), F32)
    hp_ref[0, 0:HALO, :] = jnp.where(has_prev, glu(prev_ref), zeros)
    hp_ref[0, HALO:HALO + CHUNK, :] = glu(cur_ref)
    hp_ref[0, HALO + CHUNK:, :] = jnp.where(has_next, glu(next_ref), zeros)
    up_ref[0:HALO, :] = jnp.where(has_prev, pool_in(prev_ref), zeros)
    up_ref[HALO:HALO + CHUNK, :] = pool_in(cur_ref)
    up_ref[HALO + CHUNK:, :] = jnp.where(has_next, pool_in(next_ref), zeros)

    n_shift = CHUNK + 2 * HALO - SUBLANES
    for s in range(1, SUBLANES):
        hp_ref[s, 0:n_shift, :] = hp_ref[0, s:s + n_shift, :]

    rb = 128
    for cb in range(cw // 128):
        cs = slice(cb * 128, (cb + 1) * 128)
        for r0 in range(0, CHUNK, rb):
            acc = None
            for tap in range(CONV_K):
                off = HALO - CONV_K // 2 + r0 + tap
                base = off - off % SUBLANES
                term = hp_ref[off % SUBLANES, base:base + rb, cs] * wdw_ref[tap:tap + 1, cs]
                acc = term if acc is None else acc + term
            conv_ref[r0:r0 + rb, cs] = acc + bdw_ref[:, cs]

    hc = conv_ref[...]
    mu = jnp.mean(hc, axis=-1, keepdims=True)
    d = hc - mu
    var = jnp.mean(d * d, axis=-1, keepdims=True)
    y = _silu(d * lax.rsqrt(var + EPS) * lng_ref[...] + lnb_ref[...])
    cm = jnp.dot(y.astype(BF16), wpw_ref[...], preferred_element_type=F32) + bpw_ref[...]
    o_ref[:, 0:cw] = cm.astype(BF16)

    seq_len = jnp.where(is_ctx, SEQ, DEC_SEQ)
    pos = jnp.where(is_ctx, 0, j * CHUNK) + lax.broadcasted_iota(jnp.int32, (CHUNK, 1), 0)
    for gi, w in enumerate(POOL_WINDOWS):
        cs = slice(gi * POOL_GROUP, (gi + 1) * POOL_GROUP)
        total = None
        for dlt in range(-(w // 2), w - w // 2):
            term = up_ref[HALO + dlt:HALO + dlt + CHUNK, cs]
            total = term if total is None else total + term
        lo = jnp.clip(pos - w // 2, 0, seq_len)
        hi = jnp.clip(pos - w // 2 + w, 0, seq_len)
        diff = total / (hi - lo).astype(F32) - up_ref[HALO:HALO + CHUNK, cs]
        y = jnp.dot(diff.astype(BF16), wpool_ref[gi], preferred_element_type=F32)
        o_ref[:, cw + gi * POOL_GROUP:cw + (gi + 1) * POOL_GROUP] = (
            y * pscale_ref[:, cs]).astype(BF16)


def _mixer(u, w_dw, b_dw, ln_g, ln_b, w_pw, b_pw, w_pool, pool_scale, cat, l):
    per = CHUNK // HALO
    n_halo = T_ALL // HALO
    return pl.pallas_call(
        _mixer_kernel,
        grid=(T_ALL // CHUNK,),
        in_specs=[
            pl.BlockSpec((HALO, MIX_IN), lambda i: (jnp.maximum(i * per - 1, 0), 0)),
            pl.BlockSpec((CHUNK, MIX_IN), lambda i: (i, 0)),
            pl.BlockSpec((HALO, MIX_IN), lambda i: (jnp.minimum((i + 1) * per, n_halo - 1), 0)),
            pl.BlockSpec((None, CONV_K, CONV_WIDTH), lambda i: (l, 0, 0)),
            _layer_vec_spec(l, CONV_WIDTH),
            _layer_vec_spec(l, CONV_WIDTH),
            _layer_vec_spec(l, CONV_WIDTH),
            pl.BlockSpec((None, CONV_WIDTH, CONV_WIDTH), lambda i: (l, 0, 0)),
            _layer_vec_spec(l, CONV_WIDTH),
            pl.BlockSpec((None, len(POOL_WINDOWS), POOL_GROUP, POOL_GROUP),
                         lambda i: (l, 0, 0, 0)),
            _layer_vec_spec(l, POOL_WIDTH),
            pl.BlockSpec(memory_space=pl.ANY),
        ],
        out_specs=pl.BlockSpec((CHUNK, CONV_WIDTH + POOL_WIDTH), lambda i: (i, 1)),
        out_shape=jax.ShapeDtypeStruct((T_ALL, D_MODEL), BF16),
        scratch_shapes=[
            pltpu.VMEM((SUBLANES, CHUNK + 2 * HALO, CONV_WIDTH), F32),
            pltpu.VMEM((CHUNK + 2 * HALO, POOL_WIDTH), F32),
            pltpu.VMEM((CHUNK, CONV_WIDTH), F32),
        ],
        input_output_aliases={11: 0},
        compiler_params=_params(1),
    )(u, u, u, w_dw, b_dw, ln_g, ln_b, w_pw, b_pw, w_pool, pool_scale, cat)


def _outproj_kernel(*refs, split):
    cat_ref, w_ref = refs[:2]
    gpost_ref, g1_ref, gpre_ref, sh_ref, sc_ref, xo_ref, ho_ref = refs[-7:]
    if split:
        x = _split_tile(refs[2], refs[3], pl.program_id(0), TM)
    else:
        x = refs[2][...]
    o = jnp.dot(cat_ref[...], w_ref[...], preferred_element_type=F32)
    x1 = x + g1_ref[...] * _rms(o, gpost_ref[...])
    xo_ref[...] = x1
    ho_ref[...] = (_rms(x1, gpre_ref[...]) * (1 + sc_ref[...]) + sh_ref[...]).astype(BF16)


def _outproj(cat, w_out, xs, g_post_mix, g_pre_ffn, mods, l):
    row = pl.BlockSpec((TM, D_MODEL), lambda i: (i, 0))
    split = len(xs) == 2
    return pl.pallas_call(
        functools.partial(_outproj_kernel, split=split),
        grid=(T_ALL // TM,),
        in_specs=[
            row,
            pl.BlockSpec((None, D_MODEL, D_MODEL), lambda i: (l, 0, 0),
                         pipeline_mode=pl.Buffered(1)),
        ] + (_split_specs(TM) if split else [row]) + [
            _layer_vec_spec(l, D_MODEL),
            _mod_spec(l, 2, TM),
            _layer_vec_spec(l, D_MODEL),
            _mod_spec(l, 3, TM),
            _mod_spec(l, 4, TM),
        ],
        out_specs=[row, row],
        out_shape=[
            jax.ShapeDtypeStruct((T_ALL, D_MODEL), F32),
            jax.ShapeDtypeStruct((T_ALL, D_MODEL), BF16),
        ],
        compiler_params=_params(1),
    )(cat, w_out, *xs, g_post_mix, mods, g_pre_ffn, mods, mods)


def _ffn_kernel(*refs, with_next, n_tiles):
    if with_next:
        (h_ref, wgu_ref, wo_ref, x_ref, gpost_ref, g2_ref, gpre_ref, sh_ref, sc_ref,
         xo_ref, ho_ref, acc_ref) = refs
    else:
        h_ref, wgu_ref, wo_ref, x_ref, gpost_ref, g2_ref, xo_ref, acc_ref = refs
    i = pl.program_id(0)
    f = pl.program_id(1)
    cur = i % 2

    def epilogue_chunk():
        r0 = pl.multiple_of(jnp.minimum(f, TM // EPI_ROWS - 1) * EPI_ROWS, EPI_ROWS)
        rows = pl.ds(r0, EPI_ROWS)
        x2 = x_ref[rows, :] + g2_ref[...] * _rms(acc_ref[1 - cur, rows, :], gpost_ref[...])
        xo_ref[rows, :] = x2
        if with_next:
            ho_ref[rows, :] = (
                _rms(x2, gpre_ref[...]) * (1 + sc_ref[...]) + sh_ref[...]).astype(BF16)

    @pl.when(i < n_tiles)
    def _():
        @pl.when(f == 0)
        def _():
            acc_ref[cur] = jnp.zeros((TM, D_MODEL), F32)

        @pl.when(jnp.logical_and(f == 0, i == 0))
        def _():
            acc_ref[1] = jnp.zeros((TM, D_MODEL), F32)

        epilogue_chunk()
        gu = jnp.dot(h_ref[...], wgu_ref[...], preferred_element_type=F32)
        act = (_silu(gu[:, :TF]) * gu[:, TF:]).astype(BF16)
        for c0 in range(0, D_MODEL, FFN_OUT_COLS):
            cols = slice(c0, c0 + FFN_OUT_COLS)
            acc_ref[cur, :, cols] += jnp.dot(act, wo_ref[:, cols], preferred_element_type=F32)

    @pl.when(i == n_tiles)
    def _():
        epilogue_chunk()


def _ffn(h, w_gu, w_ffn_out, x, g_post_ffn, g_pre_mix, mods, l, tile0, n_tiles):
    with_next = l + 1 < DEPTH
    nf = D_FF // TF
    assert TM // EPI_ROWS <= nf
    cur_tile = lambda i: tile0 + jnp.minimum(i, n_tiles - 1)
    prev_out = lambda i: jnp.maximum(i - 1, 0)
    prev_tile = lambda i: tile0 + prev_out(i)
    w_step = lambda i, f: jnp.where(i == n_tiles, nf - 1, f)
    out_row = pl.BlockSpec((TM, D_MODEL), lambda i, f: (prev_out(i), 0))
    in_specs = [
        pl.BlockSpec((TM, D_MODEL), lambda i, f: (cur_tile(i), 0)),
        pl.BlockSpec((None, None, D_MODEL, 2 * TF), lambda i, f: (l, w_step(i, f), 0, 0)),
        pl.BlockSpec((None, TF, D_MODEL), lambda i, f: (l, w_step(i, f), 0)),
        pl.BlockSpec((TM, D_MODEL), lambda i, f: (prev_tile(i), 0)),
        _layer_vec_spec(l, D_MODEL),
        _mod_spec(l, 5, TM, prev_tile),
    ]
    args = [h, w_gu, w_ffn_out, x, g_post_ffn, mods]
    out_specs = [out_row]
    out_shape = [jax.ShapeDtypeStruct((n_tiles * TM, D_MODEL), F32)]
    if with_next:
        in_specs += [_layer_vec_spec(l + 1, D_MODEL), _mod_spec(l + 1, 0, TM, prev_tile),
                     _mod_spec(l + 1, 1, TM, prev_tile)]
        args += [g_pre_mix, mods, mods]
        out_specs.append(out_row)
        out_shape.append(jax.ShapeDtypeStruct((n_tiles * TM, D_MODEL), BF16))
    outs = pl.pallas_call(
        functools.partial(_ffn_kernel, with_next=with_next, n_tiles=n_tiles),
        grid=(n_tiles + 1, nf),
        in_specs=in_specs,
        out_specs=out_specs,
        out_shape=out_shape,
        scratch_shapes=[pltpu.VMEM((2, TM, D_MODEL), F32)],
        compiler_params=_params(2),
    )(*args)
    return outs if with_next else (outs[0], None)


def kernel(x_prompt, x_sample, cache_k, cache_v, c, c_ctx, w_ada, b_ada, g_pre_mix, g_post_mix,
           g_pre_ffn, g_post_ffn, w_in, rpb, w_dw, b_dw, ln_conv_g, ln_conv_b, w_pw, b_pw,
           w_pool, pool_scale, w_out, w_ffn_in, w_ffn_out):
    xs = (x_prompt.reshape(T_CTX, D_MODEL), x_sample.reshape(T_LAT, D_MODEL))
    cv = jnp.concatenate(
        [c_ctx[None, :], c, jnp.zeros((MOD_ROWS - 1 - DEC_BATCH, D_MODEL), F32)])
    mods = _modulations(cv, w_ada, b_ada)
    tables = _bias_tables(rpb)

    vec = lambda a: a.reshape(DEPTH, 1, a.shape[-1])
    g_pre_mix, g_post_mix, g_pre_ffn, g_post_ffn = map(
        vec, (g_pre_mix, g_post_mix, g_pre_ffn, g_post_ffn))
    b_dw, ln_conv_g, ln_conv_b, b_pw, pool_scale = map(
        vec, (b_dw, ln_conv_g, ln_conv_b, b_pw, pool_scale))
    w_dw = w_dw.reshape(DEPTH, CONV_K, CONV_WIDTH)
    w_in, w_out, w_ffn_in, w_ffn_out, w_pw, w_pool = (
        w.astype(BF16) for w in (w_in, w_out, w_ffn_in, w_ffn_out, w_pw, w_pool))
    nf = D_FF // TF
    w_gu = w_ffn_in.reshape(DEPTH, D_MODEL, 2, nf, TF).transpose(0, 3, 1, 2, 4).reshape(
        DEPTH, nf, D_MODEL, 2 * TF)
    cache_k = cache_k.reshape(DEC_BATCH, DEPTH, PAST_LEN, ATT_WIDTH)
    cache_v = cache_v.reshape(DEC_BATCH, DEPTH, PAST_LEN, ATT_WIDTH)

    h = _prenorm(*xs, g_pre_mix, mods, 0)
    caches = None
    for l in range(DEPTH):
        q, k, v, u, *caches = _inproj(h, w_in, caches, l)
        cat = _ctx_attention(q, k, v)
        cat = _lat_attention(q, k, v, cache_k, cache_v, tables, cat, l)
        cat = _mixer(u, w_dw, b_dw, ln_conv_g, ln_conv_b, w_pw, b_pw, w_pool, pool_scale, cat, l)
        x, h2 = _outproj(cat, w_out, xs, g_post_mix, g_pre_ffn, mods, l)
        ffn = functools.partial(_ffn, h2, w_gu, w_ffn_out, x, g_post_ffn, g_pre_mix, mods, l)
        if l + 1 < DEPTH:
            x, h = ffn(0, T_ALL // TM)
            xs = (x,)

    y_prompt = ffn(0, T_CTX // TM)[0].reshape(BATCH, SEQ, D_MODEL)
    y_sample = ffn(T_CTX // TM, T_LAT // TM)[0].reshape(DEC_BATCH, DEC_SEQ, D_MODEL)
    new_k, new_v = (a.reshape(BATCH, DEPTH, SEQ, N_HEADS, HEAD_DIM) for a in caches)
    return (y_prompt, y_sample, new_k, new_v)
```
